```python
import jax, jax.numpy as jnp
from jax import lax
import numpy as np

D_MODEL = 1024
BATCH = 8
SEQ = 2048
DEPTH = 2
DEC_BATCH = 128
DEC_SEQ = 4
PAST_LEN = 16384
PAGE_SIZE = 128

M_HEADS = 4
M_HEAD_DIM = D_MODEL // M_HEADS
D_M = M_HEADS * M_HEAD_DIM
CHUNK = 64
D_CONV = D_MODEL // 2
CONV_WIDTH = 31
POOL_WINDOWS = (2, 4, 8, 16)
N_POOL_GROUPS = 4
POOL_GROUP_DIM = D_MODEL // 8
D_POOL = N_POOL_GROUPS * POOL_GROUP_DIM
POOL_OUT_DIM = D_MODEL // N_POOL_GROUPS
POOL_HIST = max(POOL_WINDOWS) - 1
N_BRANCHES = 3
D_FF = -(-8 * D_MODEL // (3 * 256)) * 256
D_IN = 4 * D_M + 2 * M_HEADS + 2 * D_CONV + D_POOL + N_BRANCHES * D_MODEL
NORM_EPS = 1e-6
LN_EPS = 1e-5

kernel_name = 'hybrid_mlstm_conv_pool_decode_step'


def rms_norm(x, g):
    xf = x.astype(jnp.float32)
    y = xf * lax.rsqrt(jnp.mean(xf * xf, axis=-1, keepdims=True) + NORM_EPS)
    return (y * g.astype(jnp.float32)).astype(x.dtype)


def layer_norm(x, g, b):
    xf = x.astype(jnp.float32)
    mu = jnp.mean(xf, axis=-1, keepdims=True)
    var = jnp.mean(jnp.square(xf - mu), axis=-1, keepdims=True)
    y = (xf - mu) * lax.rsqrt(var + LN_EPS) * g.astype(jnp.float32) + b.astype(jnp.float32)
    return y.astype(x.dtype)


def mlstm_chunk(carry, inp):
    C0, n0, m0 = carry
    q, k, v, it, lf = inp
    L = q.shape[2]
    b = jnp.cumsum(lf, axis=-1)
    a = it - b
    m = jnp.maximum(b + m0[..., None], b + lax.cummax(a, axis=2))
    causal = jnp.tril(jnp.ones((L, L), dtype=bool))
    logw = b[..., :, None] + a[..., None, :] - m[..., :, None]
    w = jnp.exp(jnp.where(causal, logw, -jnp.inf))
    g = jnp.exp(b + m0[..., None] - m)
    s = jnp.einsum('bhtd,bhsd->bhts', q, k) * w
    num = g[..., None] * jnp.einsum('bhvd,bhtd->bhtv', C0, q) + jnp.einsum('bhts,bhsv->bhtv', s, v)
    den = g * jnp.einsum('bhd,bhtd->bht', n0, q) + jnp.sum(s, axis=-1)
    h = num / jnp.maximum(jnp.abs(den), jnp.exp(-m))[..., None]
    bL = b[..., -1]
    mL = m[..., -1]
    decay = jnp.exp(bL[..., None] + a - mL[..., None])
    keep = jnp.exp(bL + m0 - mL)
    C1 = keep[..., None, None] * C0 + jnp.einsum('bhs,bhsv,bhsd->bhvd', decay, v, k)
    n1 = keep[..., None] * n0 + jnp.einsum('bhs,bhsd->bhd', decay, k)
    return (C1, n1, mL), h


def mlstm_scan(q, k, v, it, lf, C0, n0, m0):
    Bsz, T = q.shape[0], q.shape[1]
    L = CHUNK if T % CHUNK == 0 else T
    nc = T // L

    def chunks4(z):
        return z.reshape(Bsz, nc, L, M_HEADS, M_HEAD_DIM).transpose(1, 0, 3, 2, 4)

    def chunks3(z):
        return z.reshape(Bsz, nc, L, M_HEADS).transpose(1, 0, 3, 2)

    (C1, n1, m1), h = lax.scan(mlstm_chunk, (C0, n0, m0),
                               (chunks4(q), chunks4(k), chunks4(v), chunks3(it), chunks3(lf)))
    h = h.transpose(1, 0, 3, 2, 4).reshape(Bsz, T, D_M)
    return h, C1, n1, m1


def trunk_layer(x, C0, n0, m0, conv_buf, pool_buf, pos0,
                norm_mix_g, w_in, b_igate, b_fgate, w_mlstm_out,
                conv_dw_w, conv_dw_b, conv_ln_g, conv_ln_b, w_conv_out,
                w_pool, pool_scale, w_out, norm_ffn_g, w_ffn_gate, w_ffn_up, w_ffn_down):
    f32 = jnp.float32
    Bsz, T, _ = x.shape
    xn = rms_norm(x, norm_mix_g)
    proj = xn @ w_in
    sizes = [D_M, D_M, D_M, D_M, M_HEADS, M_HEADS, 2 * D_CONV, D_POOL, N_BRANCHES * D_MODEL]
    idx, acc = [], 0
    for sz in sizes[:-1]:
        acc += sz
        idx.append(acc)
    q, k, v, o, ig, fg, cv, pl, gt = jnp.split(proj, idx, axis=-1)

    def heads(z):
        return z.reshape(Bsz, T, M_HEADS, M_HEAD_DIM).astype(f32)
    it = ig.astype(f32) + b_igate.astype(f32)
    lf = jax.nn.log_sigmoid(fg.astype(f32) + b_fgate.astype(f32))
    h, C1, n1, m1 = mlstm_scan(heads(q), heads(k) * (M_HEAD_DIM ** -0.5), heads(v), it, lf,
                               C0.astype(f32), n0.astype(f32), m0.astype(f32))
    h = (jax.nn.sigmoid(o.astype(f32)) * h).astype(x.dtype)
    out_m = h @ w_mlstm_out

    ca, cb = jnp.split(cv, 2, axis=-1)
    u = ca * jax.nn.sigmoid(cb)
    ext = jnp.concatenate([conv_buf.astype(u.dtype), u], axis=1)
    dw = lax.conv_general_dilated(ext, conv_dw_w[:, None, :].astype(ext.dtype),
                                  window_strides=(1,), padding='VALID',
                                  dimension_numbers=('NWC', 'WIO', 'NWC'),
                                  feature_group_count=D_CONV) + conv_dw_b
    zc = jax.nn.silu(layer_norm(dw, conv_ln_g, conv_ln_b))
    out_c = zc @ w_conv_out
    new_conv = ext[:, -(CONV_WIDTH - 1):]

    pext = jnp.concatenate([pool_buf.astype(pl.dtype), pl], axis=1)
    cs = jnp.pad(jnp.cumsum(pext.astype(f32), axis=1), ((0, 0), (1, 0), (0, 0)))
    pos = pos0 + jnp.arange(T)
    means = []
    for gi, win in enumerate(POOL_WINDOWS):
        sl = slice(gi * POOL_GROUP_DIM, (gi + 1) * POOL_GROUP_DIM)
        end = cs[:, POOL_HIST + 1:POOL_HIST + 1 + T, sl]
        start = cs[:, POOL_HIST + 1 - win:POOL_HIST + 1 - win + T, sl]
        cnt = jnp.minimum(pos + 1, win).astype(f32)[None, :, None]
        means.append((end - start) / cnt)
    mean = jnp.stack(means, axis=2)
    pooled = (mean - pl.astype(f32).reshape(Bsz, T, N_POOL_GROUPS, POOL_GROUP_DIM)).astype(x.dtype)
    out_p = jnp.einsum('btgc,gce->btge', pooled, w_pool).reshape(Bsz, T, D_MODEL) * pool_scale
    new_pool = pext[:, -POOL_HIST:]

    g_m, g_c, g_p = jnp.split(jax.nn.sigmoid(gt), N_BRANCHES, axis=-1)
    x = x + (g_m * out_m + g_c * out_c + g_p * out_p) @ w_out

    xf = rms_norm(x, norm_ffn_g)
    x = x + (jax.nn.silu(xf @ w_ffn_gate) * (xf @ w_ffn_up)) @ w_ffn_down
    return x, C1, n1, m1, new_conv, new_pool


def setup_inputs(seed: int = 0) -> dict:
    key = jax.random.key(seed)
    ks = jax.random.split(key, 32)
    f32 = jnp.float32

    def nrm(k, shape, s):
        return s * jax.random.normal(k, shape, f32)

    L = DEPTH
    return {
        'x_prompt': nrm(ks[0], (BATCH, SEQ, D_MODEL), 1.0),
        'x_sample': nrm(ks[1], (DEC_BATCH, DEC_SEQ, D_MODEL), 1.0),
        'state_mlstm_C': nrm(ks[2], (L, DEC_BATCH, M_HEADS, M_HEAD_DIM, M_HEAD_DIM), 0.3),
        'state_mlstm_n': nrm(ks[3], (L, DEC_BATCH, M_HEADS, M_HEAD_DIM), 0.3),
        'state_mlstm_m': nrm(ks[4], (L, DEC_BATCH, M_HEADS), 1.0),
        'cache_conv': nrm(ks[5], (L, DEC_BATCH, CONV_WIDTH - 1, D_CONV), 0.5),
        'cache_pool': nrm(ks[6], (L, DEC_BATCH, POOL_HIST, D_POOL), 1.0),
        'norm_mix_g': 1.0 + nrm(ks[7], (L, D_MODEL), 0.05),
        'w_in': nrm(ks[8], (L, D_MODEL, D_IN), D_MODEL ** -0.5),
        'b_igate': nrm(ks[9], (L, M_HEADS), 0.1),
        'b_fgate': 3.0 + nrm(ks[10], (L, M_HEADS), 0.5),
        'w_mlstm_out': nrm(ks[11], (L, D_M, D_MODEL), D_M ** -0.5),
        'conv_dw_w': nrm(ks[12], (L, CONV_WIDTH, D_CONV), CONV_WIDTH ** -0.5),
        'conv_dw_b': nrm(ks[13], (L, D_CONV), 0.01),
        'conv_ln_g': 1.0 + nrm(ks[14], (L, D_CONV), 0.05),
        'conv_ln_b': nrm(ks[15], (L, D_CONV), 0.01),
        'w_conv_out': nrm(ks[16], (L, D_CONV, D_MODEL), D_CONV ** -0.5),
        'w_pool': nrm(ks[17], (L, N_POOL_GROUPS, POOL_GROUP_DIM, POOL_OUT_DIM), POOL_GROUP_DIM ** -0.5),
        'pool_scale': 1.0 + nrm(ks[18], (L, D_MODEL), 0.1),
        'w_out': nrm(ks[19], (L, D_MODEL, D_MODEL), D_MODEL ** -0.5),
        'norm_ffn_g': 1.0 + nrm(ks[20], (L, D_MODEL), 0.05),
        'w_ffn_gate': nrm(ks[21], (L, D_MODEL, D_FF), D_MODEL ** -0.5),
        'w_ffn_up': nrm(ks[22], (L, D_MODEL, D_FF), D_MODEL ** -0.5),
        'w_ffn_down': nrm(ks[23], (L, D_FF, D_MODEL), D_FF ** -0.5),
        'norm_final_g': 1.0 + nrm(ks[24], (D_MODEL,), 0.05),
    }


def reference(x_prompt, x_sample, state_mlstm_C, state_mlstm_n, state_mlstm_m, cache_conv, cache_pool,
              norm_mix_g, w_in, b_igate, b_fgate, w_mlstm_out, conv_dw_w, conv_dw_b, conv_ln_g, conv_ln_b,
              w_conv_out, w_pool, pool_scale, w_out, norm_ffn_g, w_ffn_gate, w_ffn_up, w_ffn_down,
              norm_final_g):
    f32 = jnp.float32
    C0p = jnp.zeros((BATCH, M_HEADS, M_HEAD_DIM, M_HEAD_DIM), f32)
    n0p = jnp.zeros((BATCH, M_HEADS, M_HEAD_DIM), f32)
    m0p = jnp.zeros((BATCH, M_HEADS), f32)
    conv0p = jnp.zeros((BATCH, CONV_WIDTH - 1, D_CONV), x_prompt.dtype)
    pool0p = jnp.zeros((BATCH, POOL_HIST, D_POOL), x_prompt.dtype)

    yp, ys = x_prompt, x_sample
    pC, pn, pm, pcv, ppl = [], [], [], [], []
    sC, sn, sm, scv, spl = [], [], [], [], []
    for l in range(DEPTH):
        params = (norm_mix_g[l], w_in[l], b_igate[l], b_fgate[l], w_mlstm_out[l],
                  conv_dw_w[l], conv_dw_b[l], conv_ln_g[l], conv_ln_b[l], w_conv_out[l],
                  w_pool[l], pool_scale[l], w_out[l], norm_ffn_g[l], w_ffn_gate[l], w_ffn_up[l],
                  w_ffn_down[l])
        yp, c1, n1, m1, cv1, pl1 = trunk_layer(yp, C0p, n0p, m0p, conv0p, pool0p, 0, *params)
        pC.append(c1); pn.append(n1); pm.append(m1); pcv.append(cv1); ppl.append(pl1)
        ys, c2, n2, m2, cv2, pl2 = trunk_layer(ys, state_mlstm_C[l], state_mlstm_n[l], state_mlstm_m[l],
                                               cache_conv[l], cache_pool[l], PAST_LEN, *params)
        sC.append(c2); sn.append(n2); sm.append(m2); scv.append(cv2); spl.append(pl2)

    y_prompt = rms_norm(yp, norm_final_g)
    y_sample = rms_norm(ys, norm_final_g)
    return (y_prompt, y_sample,
            jnp.stack(pC), jnp.stack(pn), jnp.stack(pm), jnp.stack(pcv), jnp.stack(ppl),
            jnp.stack(sC), jnp.stack(sn), jnp.stack(sm), jnp.stack(scv), jnp.stack(spl))
```

```python
import functools

import jax
import jax.numpy as jnp
from jax import lax
from jax.experimental import pallas as pl
from jax.experimental.pallas import tpu as pltpu

f32 = jnp.float32
bf16 = jnp.bfloat16

D_MODEL = 1024
M_HEADS = 4
M_HEAD_DIM = 256
D_CONV = 512
CONV_WIDTH = 31
CONV_HIST = CONV_WIDTH - 1
POOL_WINDOWS = (2, 4, 8, 16)
POOL_GROUP_DIM = 128
POOL_OUT_DIM = 256
D_POOL = 512
POOL_HIST = 15
D_FF = 2816
PAST_LEN = 16384
NORM_EPS = 1e-6
LN_EPS = 1e-5

GATE_LANES = 128
SAMPLE_PAD = 8
ROW_TILE = 512
CONV_PAD = 32
POOL_PAD = 16
VMEM_LIMIT = 56 * 1024 * 1024

NT_DIMS = (((1,), (1,)), ((), ()))

_C_CV = 4 * D_MODEL
_C_PL = _C_CV + 2 * D_CONV
_C_GT = _C_PL + D_POOL
_W_MAIN = _C_GT + 3 * D_MODEL


def _dot(a, b):
    return jnp.dot(a, b, preferred_element_type=f32)


def _rms(x, g):
    return x * lax.rsqrt(jnp.mean(x * x, axis=-1, keepdims=True) + NORM_EPS) * g


def _resident(shape):
    zeros = (0,) * len(shape)
    return pl.BlockSpec(shape, lambda *_: zeros, pipeline_mode=pl.Buffered(1))


def _params(*sem):
    return pltpu.CompilerParams(dimension_semantics=sem, vmem_limit_bytes=VMEM_LIMIT)


def _inproj_body(x_ref, g_ref, w_ref, wgh_ref, wgl_ref, gb_ref,
                 qkv_ref, og_ref, u_ref, pl_ref, sg_ref, gates_ref):
    xn = _rms(x_ref[...], g_ref[...])
    xb = xn.astype(bf16)
    xl = (xn - xb.astype(f32)).astype(bf16)
    z = (_dot(xb, wgh_ref[...]) + _dot(xb, wgl_ref[...]) + _dot(xl, wgh_ref[...])
         + gb_ref[...])
    lane = lax.broadcasted_iota(jnp.int32, z.shape, 1)
    log_f = jnp.minimum(z, 0.0) - jnp.log1p(jnp.exp(-jnp.abs(z)))
    gates_ref[...] = jnp.where(lane < M_HEADS, z, log_f)

    for c in range(3):
        r = _dot(xb, w_ref[:, c * D_MODEL:(c + 1) * D_MODEL])
        if c == 1:
            r = r * (M_HEAD_DIM ** -0.5)
        qkv_ref[:, c * D_MODEL:(c + 1) * D_MODEL] = r.astype(bf16)
    og_ref[...] = jax.nn.sigmoid(_dot(xb, w_ref[:, 3 * D_MODEL:4 * D_MODEL])).astype(bf16)
    ca = _dot(xb, w_ref[:, _C_CV:_C_CV + D_CONV])
    cb = _dot(xb, w_ref[:, _C_CV + D_CONV:_C_PL])
    u_ref[...] = ca * jax.nn.sigmoid(cb)
    pl_ref[...] = _dot(xb, w_ref[:, _C_PL:_C_GT])
    for c in range(3):
        r = _dot(xb, w_ref[:, _C_GT + c * D_MODEL:_C_GT + (c + 1) * D_MODEL])
        sg_ref[:, c * D_MODEL:(c + 1) * D_MODEL] = jax.nn.sigmoid(r).astype(bf16)


def _inproj(x, g, w_main, wg_hi, wg_lo, gate_bias):
    rows = x.shape[0]
    tm = ROW_TILE
    row = lambda width: pl.BlockSpec((tm, width), lambda i: (i, 0))
    return pl.pallas_call(
        _inproj_body,
        grid=(rows // tm,),
        in_specs=[row(D_MODEL), _resident((1, D_MODEL)), _resident((D_MODEL, _W_MAIN)),
                  _resident((D_MODEL, GATE_LANES)), _resident((D_MODEL, GATE_LANES)),
                  _resident((1, GATE_LANES))],
        out_specs=[row(3 * D_MODEL), row(D_MODEL), row(D_CONV), row(D_POOL),
                   row(3 * D_MODEL), row(GATE_LANES)],
        out_shape=[jax.ShapeDtypeStruct((rows, 3 * D_MODEL), bf16),
                   jax.ShapeDtypeStruct((rows, D_MODEL), bf16),
                   jax.ShapeDtypeStruct((rows, D_CONV), f32),
                   jax.ShapeDtypeStruct((rows, D_POOL), f32),
                   jax.ShapeDtypeStruct((rows, 3 * D_MODEL), bf16),
                   jax.ShapeDtypeStruct((rows, GATE_LANES), f32)],
        compiler_params=_params("arbitrary"),
        name="inproj",
    )(x, g, w_main, wg_hi, wg_lo, gate_bias)


def _mlstm_body(q_ref, k_ref, v_ref, og_ref, gates_ref, c0_ref, n0_ref, m0_ref,
                h_ref, c_ref, n_ref, m_ref, *, n_seq, seq_rows, valid, n_chunks):
    rows = n_seq * seq_rows
    shift = seq_rows.bit_length() - 1
    head = pl.program_id(1)
    if n_chunks > 1:
        @pl.when(pl.program_id(2) == 0)
        def _():
            c_ref[...] = c0_ref[...]
            n_ref[...] = n0_ref[...]
            m_ref[...] = m0_ref[...]
        cs, ns, ms = c_ref, n_ref, m_ref
    else:
        cs, ns, ms = c0_ref, n0_ref, m0_ref

    ri = lax.broadcasted_iota(jnp.int32, (rows, rows), 0)
    ci = lax.broadcasted_iota(jnp.int32, (rows, rows), 1)
    same = (ri >> shift) == (ci >> shift)
    causal = same & (ci <= ri)
    eye = ri == ci
    last_col = same & ((ci & (seq_rows - 1)) == seq_rows - 1)

    gts = gates_ref[...]
    lane = lax.broadcasted_iota(jnp.int32, gts.shape, 1)
    t_in = lax.broadcasted_iota(jnp.int32, gts.shape, 0) & (seq_rows - 1)
    ok = t_in < valid
    b_all = jnp.where(ok, gts, 0.0)
    step = 1
    while step < seq_rows:
        b_all = b_all + jnp.where(t_in >= step, pltpu.roll(b_all, step, 0), 0.0)
        step *= 2
    i_gate = jnp.sum(jnp.where(ok & (lane == head), gts, 0.0), axis=1, keepdims=True)
    i_gate = jnp.where(ok[:, 0:1], i_gate, -jnp.inf)
    b = jnp.sum(jnp.where(lane == head + M_HEADS, b_all, 0.0), axis=1, keepdims=True)
    a = i_gate - b

    a_row = jnp.sum(jnp.where(eye, a, 0.0), axis=0, keepdims=True)
    amat = jnp.where(causal, a_row, -jnp.inf)
    cmax = jnp.max(amat, axis=1, keepdims=True)
    if n_seq == 1:
        m0_rows = ms[0, 0, 0:1, 0:1]
    else:
        m0_rows = ms[...].reshape(rows, GATE_LANES)[:, 0:1]
    mx = jnp.maximum(m0_rows, cmax)
    m = b + mx
    w = jnp.exp(amat - mx)
    g = jnp.exp(m0_rows - mx)

    q = q_ref[...]
    k = k_ref[...]
    v = v_ref[...]
    s = lax.dot_general(q, k, NT_DIMS, preferred_element_type=f32) * w
    intra = _dot(s.astype(bf16), v)
    den_intra = jnp.sum(s, axis=1, keepdims=True)

    d = -mx
    d_row = jnp.sum(jnp.where(eye, d, 0.0), axis=0, keepdims=True)
    d_last = jnp.sum(jnp.where(last_col, d_row, 0.0), axis=1, keepdims=True)
    decay = jnp.exp(d_last + a)
    qf = q.astype(f32)
    kf = k.astype(f32)
    vd_t = (decay * v.astype(f32)).T.astype(bf16)
    dk = decay * kf
    row_id = lax.broadcasted_iota(jnp.int32, (rows, M_HEAD_DIM), 0) >> shift

    inter, qn = [], []
    for i in range(n_seq):
        lo, hi = i * seq_rows, (i + 1) * seq_rows
        c0 = cs[i, 0]
        n0 = ns[i, 0]
        m0 = ms[i, 0, 0:1, 0:1]
        res = lax.dot_general(q, c0.astype(bf16), NT_DIMS, preferred_element_type=f32)
        inter.append(res[lo:hi])
        qn.append(jnp.sum(qf[lo:hi] * n0, axis=1, keepdims=True))
        keep = jnp.exp(d[hi - 1:hi] + m0)
        k_i = k if n_seq == 1 else jnp.where(row_id == i, kf, 0.0).astype(bf16)
        c_ref[i, 0] = keep * c0 + _dot(vd_t, k_i)
        n_ref[i, 0] = keep * n0 + jnp.sum(dk[lo:hi], axis=0, keepdims=True)
        m_ref[i, 0] = jnp.broadcast_to(m[hi - 1:hi], (SAMPLE_PAD, GATE_LANES))
    inter = inter[0] if n_seq == 1 else jnp.concatenate(inter, axis=0)
    qn = qn[0] if n_seq == 1 else jnp.concatenate(qn, axis=0)

    num = g * inter + intra
    den = g * qn + den_intra
    hv = num / jnp.maximum(jnp.abs(den), jnp.exp(-m))
    h_ref[...] = (og_ref[...].astype(f32) * hv).astype(bf16)


def _mlstm(qkv, og, gates, c0, n0, m0, layer, *, batch, n_seq, seq_rows, valid, n_chunks):
    rows_total = qkv.shape[0]
    rows = n_seq * seq_rows
    nb = batch // n_seq
    hd = M_HEAD_DIM
    rmap = lambda off: (lambda b, h, c: (b * n_chunks + c, h + off))
    smap5 = lambda b, h, c: (layer, b, h, 0, 0)
    smap4 = lambda b, h, c: (b, h, 0, 0)
    body = functools.partial(_mlstm_body, n_seq=n_seq, seq_rows=seq_rows, valid=valid,
                             n_chunks=n_chunks)
    return pl.pallas_call(
        body,
        grid=(nb, M_HEADS, n_chunks),
        in_specs=[pl.BlockSpec((rows, hd), rmap(0)),
                  pl.BlockSpec((rows, hd), rmap(M_HEADS)),
                  pl.BlockSpec((rows, hd), rmap(2 * M_HEADS)),
                  pl.BlockSpec((rows, hd), rmap(0)),
                  pl.BlockSpec((rows, GATE_LANES), lambda b, h, c: (b * n_chunks + c, 0)),
                  pl.BlockSpec((None, n_seq, 1, hd, hd), smap5),
                  pl.BlockSpec((None, n_seq, 1, 1, hd), smap5),
                  pl.BlockSpec((n_seq, 1, SAMPLE_PAD, GATE_LANES), smap4)],
        out_specs=[pl.BlockSpec((rows, hd), rmap(0)),
                   pl.BlockSpec((n_seq, 1, hd, hd), smap4),
                   pl.BlockSpec((n_seq, 1, 1, hd), smap4),
                   pl.BlockSpec((n_seq, 1, SAMPLE_PAD, GATE_LANES), smap4)],
        out_shape=[jax.ShapeDtypeStruct((rows_total, D_MODEL), bf16),
                   jax.ShapeDtypeStruct((batch, M_HEADS, hd, hd), f32),
                   jax.ShapeDtypeStruct((batch, M_HEADS, 1, hd), f32),
                   jax.ShapeDtypeStruct((batch, M_HEADS, SAMPLE_PAD, GATE_LANES), f32)],
        compiler_params=_params("arbitrary", "arbitrary", "arbitrary"),
        name="mlstm",
    )(qkv, qkv, qkv, og, gates, c0, n0, m0)


def _convpool_body(u_ref, p_ref, cc_ref, pc_ref, dww_ref, dwb_ref, lng_ref, lnb_ref,
                   zc_ref, po_ref, ncv_ref, npl_ref, ext_ref, pext_ref,
                   *, n_seq, seq_rows, valid, n_tiles, pos0):
    t = pl.program_id(1)
    conv_rows = min(seq_rows, 64)
    pool_rows = min(seq_rows, 128)
    for i in range(n_seq):
        r0 = i * seq_rows
        if n_tiles > 1:
            @pl.when(t == 0)
            def _():
                ext_ref[pl.ds(CONV_PAD - CONV_HIST, CONV_HIST), :] = cc_ref[i]
                pext_ref[pl.ds(POOL_PAD - POOL_HIST, POOL_HIST), :] = pc_ref[i]

            @pl.when(t > 0)
            def _():
                ext_ref[pl.ds(0, CONV_PAD), :] = ext_ref[pl.ds(seq_rows, CONV_PAD), :]
                pext_ref[pl.ds(0, POOL_PAD), :] = pext_ref[pl.ds(seq_rows, POOL_PAD), :]
        else:
            ext_ref[pl.ds(CONV_PAD - CONV_HIST, CONV_HIST), :] = cc_ref[i]
            pext_ref[pl.ds(POOL_PAD - POOL_HIST, POOL_HIST), :] = pc_ref[i]
        ext_ref[pl.ds(CONV_PAD, seq_rows), :] = u_ref[pl.ds(r0, seq_rows), :]
        pext_ref[pl.ds(POOL_PAD, seq_rows), :] = p_ref[pl.ds(r0, seq_rows), :]

        for rs in range(0, seq_rows, conv_rows):
            acc = jnp.broadcast_to(dwb_ref[...], (conv_rows, D_CONV))
            for j in range(CONV_WIDTH):
                acc = acc + (ext_ref[pl.ds(CONV_PAD - CONV_HIST + j + rs, conv_rows), :]
                             * dww_ref[j:j + 1, :])
            mu = jnp.mean(acc, axis=-1, keepdims=True)
            xc = acc - mu
            var = jnp.mean(xc * xc, axis=-1, keepdims=True)
            y = xc * lax.rsqrt(var + LN_EPS) * lng_ref[...] + lnb_ref[...]
            zc_ref[pl.ds(r0 + rs, conv_rows), :] = (y * jax.nn.sigmoid(y)).astype(bf16)

        for rs in range(0, seq_rows, pool_rows):
            pos = (pos0 + t * seq_rows + rs
                   + lax.broadcasted_iota(jnp.int32, (pool_rows, POOL_GROUP_DIM), 0))
            for gi, win in enumerate(POOL_WINDOWS):
                cols = pl.ds(gi * POOL_GROUP_DIM, POOL_GROUP_DIM)
                cur = pext_ref[pl.ds(POOL_PAD + rs, pool_rows), cols]
                acc = cur
                for j in range(1, win):
                    acc = acc + pext_ref[pl.ds(POOL_PAD + rs - j, pool_rows), cols]
                cnt = jnp.minimum(pos + 1, win).astype(f32)
                po_ref[pl.ds(r0 + rs, pool_rows), cols] = (acc / cnt - cur).astype(bf16)

        def _caches():
            ncv_ref[i] = ext_ref[pl.ds(CONV_PAD - CONV_HIST + valid, CONV_HIST), :]
            npl_ref[i] = pext_ref[pl.ds(POOL_PAD - POOL_HIST + valid, POOL_HIST), :]
        if n_tiles > 1:
            pl.when(t == n_tiles - 1)(_caches)
        else:
            _caches()


def _convpool(u, p, cache_conv, cache_pool, layer, dww, dwb, lng, lnb,
              *, batch, n_seq, seq_rows, valid, n_tiles, pos0):
    rows_total = u.shape[0]
    rows = n_seq * seq_rows
    nb = batch // n_seq
    rmap = lambda b, t: (b * n_tiles + t, 0)
    body = functools.partial(_convpool_body, n_seq=n_seq, seq_rows=seq_rows, valid=valid,
                             n_tiles=n_tiles, pos0=pos0)
    return pl.pallas_call(
        body,
        grid=(nb, n_tiles),
        in_specs=[pl.BlockSpec((rows, D_CONV), rmap),
                  pl.BlockSpec((rows, D_POOL), rmap),
                  pl.BlockSpec((None, n_seq, CONV_HIST, D_CONV), lambda b, t: (layer, b, 0, 0)),
                  pl.BlockSpec((None, n_seq, POOL_HIST, D_POOL), lambda b, t: (layer, b, 0, 0)),
                  _resident((CONV_WIDTH, D_CONV)), _resident((1, D_CONV)),
                  _resident((1, D_CONV)), _resident((1, D_CONV))],
        out_specs=[pl.BlockSpec((rows, D_CONV), rmap),
                   pl.BlockSpec((rows, D_POOL), rmap),
                   pl.BlockSpec((n_seq, CONV_HIST, D_CONV), lambda b, t: (b, 0, 0)),
                   pl.BlockSpec((n_seq, POOL_HIST, D_POOL), lambda b, t: (b, 0, 0))],
        out_shape=[jax.ShapeDtypeStruct((rows_total, D_CONV), bf16),
                   jax.ShapeDtypeStruct((rows_total, D_POOL), bf16),
                   jax.ShapeDtypeStruct((batch, CONV_HIST, D_CONV), f32),
                   jax.ShapeDtypeStruct((batch, POOL_HIST, D_POOL), f32)],
        scratch_shapes=[pltpu.VMEM((CONV_PAD + seq_rows, D_CONV), f32),
                        pltpu.VMEM((POOL_PAD + seq_rows, D_POOL), f32)],
        compiler_params=_params("arbitrary", "arbitrary"),
        name="convpool",
    )(u, p, cache_conv, cache_pool, dww, dwb, lng, lnb)


def _merge_body(h_ref, zc_ref, po_ref, sg_ref, x_ref, wm_ref, wc_ref, wp_ref, ps_ref, wo_ref,
                o_ref):
    out_m = _dot(h_ref[...], wm_ref[...])
    merged = sg_ref[:, 0:D_MODEL].astype(f32) * out_m
    out_c = _dot(zc_ref[...], wc_ref[...])
    merged = merged + sg_ref[:, D_MODEL:2 * D_MODEL].astype(f32) * out_c
    out_p = jnp.concatenate(
        [_dot(po_ref[:, gi * POOL_GROUP_DIM:(gi + 1) * POOL_GROUP_DIM], wp_ref[gi])
         for gi in range(len(POOL_WINDOWS))], axis=1) * ps_ref[...]
    merged = merged + sg_ref[:, 2 * D_MODEL:3 * D_MODEL].astype(f32) * out_p
    o_ref[...] = x_ref[...] + _dot(merged.astype(bf16), wo_ref[...])


def _merge(h, zc, po, sg, x, wm, wc, wp, ps, wo):
    rows = x.shape[0]
    tm = ROW_TILE
    row = lambda width: pl.BlockSpec((tm, width), lambda i: (i, 0))
    return pl.pallas_call(
        _merge_body,
        grid=(rows // tm,),
        in_specs=[row(D_MODEL), row(D_CONV), row(D_POOL), row(3 * D_MODEL), row(D_MODEL),
                  _resident((D_MODEL, D_MODEL)), _resident((D_CONV, D_MODEL)),
                  _resident((len(POOL_WINDOWS), POOL_GROUP_DIM, POOL_OUT_DIM)),
                  _resident((1, D_MODEL)), _resident((D_MODEL, D_MODEL))],
        out_specs=row(D_MODEL),
        out_shape=jax.ShapeDtypeStruct((rows, D_MODEL), f32),
        compiler_params=_params("arbitrary"),
        name="merge",
    )(h, zc, po, sg, x, wm, wc, wp, ps, wo)


FF_CHUNK = D_FF // 2


def _ffn_body(x_ref, g_ref, wg_ref, wu_ref, wd_ref, gf_ref, o_ref, *, final):
    x = x_ref[...]
    xn = _rms(x, g_ref[...]).astype(bf16)
    acc = x
    for c in range(0, D_FF, FF_CHUNK):
        a = _dot(xn, wg_ref[:, c:c + FF_CHUNK])
        up = _dot(xn, wu_ref[:, c:c + FF_CHUNK])
        act = (a * jax.nn.sigmoid(a) * up).astype(bf16)
        acc = acc + _dot(act, wd_ref[c:c + FF_CHUNK, :])
    if final:
        acc = _rms(acc, gf_ref[...])
    o_ref[...] = acc


def _ffn(x, g, wg, wu, wd, gf, final):
    rows = x.shape[0]
    tm = ROW_TILE
    row = pl.BlockSpec((tm, D_MODEL), lambda i: (i, 0))
    return pl.pallas_call(
        functools.partial(_ffn_body, final=final),
        grid=(rows // tm,),
        in_specs=[row, _resident((1, D_MODEL)), _resident((D_MODEL, D_FF)),
                  _resident((D_MODEL, D_FF)), _resident((D_FF, D_MODEL)),
                  _resident((1, D_MODEL))],
        out_specs=row,
        out_shape=jax.ShapeDtypeStruct((rows, D_MODEL), f32),
        compiler_params=_params("arbitrary"),
        name="ffn",
    )(x, g, wg, wu, wd, gf)


PROMPT_CHUNK = 256
PROMPT_CONV_TILE = 512
SAMPLE_GROUP = 16


def kernel(x_prompt, x_sample, state_mlstm_C, state_mlstm_n, state_mlstm_m, cache_conv, cache_pool, norm_mix_g, w_in, b_igate, b_fgate, w_mlstm_out, conv_dw_w, conv_dw_b, conv_ln_g, conv_ln_b, w_conv_out, w_pool, pool_scale, w_out, norm_ffn_g, w_ffn_gate, w_ffn_up, w_ffn_down, norm_final_g):
    depth = w_in.shape[0]
    bp, tp, _ = x_prompt.shape
    bs, ts, _ = x_sample.shape
    hd = M_HEAD_DIM

    w_main = jnp.concatenate([w_in[:, :, :4 * D_MODEL], w_in[:, :, 4 * D_MODEL + 2 * M_HEADS:]],
                             axis=-1).astype(bf16)
    wg = jnp.pad(w_in[:, :, 4 * D_MODEL:4 * D_MODEL + 2 * M_HEADS],
                 ((0, 0), (0, 0), (0, GATE_LANES - 2 * M_HEADS)))
    wg_hi = wg.astype(bf16)
    wg_lo = (wg - wg_hi.astype(f32)).astype(bf16)
    gate_bias = jnp.pad(jnp.concatenate([b_igate, b_fgate], axis=-1),
                        ((0, 0), (0, GATE_LANES - 2 * M_HEADS)))[:, None, :]
    wm_b = w_mlstm_out.astype(bf16)
    wc_b = w_conv_out.astype(bf16)
    wp_b = w_pool.astype(bf16)
    wo_b = w_out.astype(bf16)
    wfg_b = w_ffn_gate.astype(bf16)
    wfu_b = w_ffn_up.astype(bf16)
    wfd_b = w_ffn_down.astype(bf16)
    row2 = lambda a, l: a[l][None, :]

    xp = x_prompt.reshape(bp * tp, D_MODEL)
    xs = jnp.pad(x_sample, ((0, 0), (0, SAMPLE_PAD - ts), (0, 0))).reshape(bs * SAMPLE_PAD, D_MODEL)

    zero_c = jnp.zeros((1, bp, M_HEADS, hd, hd), f32)
    zero_n = jnp.zeros((1, bp, M_HEADS, 1, hd), f32)
    zero_m = jnp.zeros((bp, M_HEADS, SAMPLE_PAD, GATE_LANES), f32)
    zero_cc = jnp.zeros((1, bp, CONV_HIST, D_CONV), f32)
    zero_pc = jnp.zeros((1, bp, POOL_HIST, D_POOL), f32)
    state_n5 = state_mlstm_n.reshape(depth, bs, M_HEADS, 1, hd)

    prompt_cfg = dict(batch=bp, n_seq=1, seq_rows=PROMPT_CHUNK, valid=PROMPT_CHUNK,
                      n_chunks=tp // PROMPT_CHUNK)
    sample_cfg = dict(batch=bs, n_seq=SAMPLE_GROUP, seq_rows=SAMPLE_PAD, valid=ts, n_chunks=1)
    prompt_cp = dict(batch=bp, n_seq=1, seq_rows=PROMPT_CONV_TILE, valid=PROMPT_CONV_TILE,
                     n_tiles=tp // PROMPT_CONV_TILE, pos0=0)
    sample_cp = dict(batch=bs, n_seq=SAMPLE_GROUP, seq_rows=SAMPLE_PAD, valid=ts, n_tiles=1,
                     pos0=PAST_LEN)

    outs = {name: [] for name in ("pC", "pn", "pm", "pcv", "ppl", "sC", "sn", "sm", "scv", "spl")}
    for l in range(depth):
        final = l == depth - 1
        conv_w = (conv_dw_w[l], row2(conv_dw_b, l), row2(conv_ln_g, l), row2(conv_ln_b, l))
        streams = (
            (xp, zero_c, zero_n, zero_m, zero_cc, zero_pc, 0, prompt_cfg, prompt_cp, "p"),
            (xs, state_mlstm_C, state_n5,
             jnp.broadcast_to(state_mlstm_m[l][:, :, None, None],
                              (bs, M_HEADS, SAMPLE_PAD, GATE_LANES)),
             cache_conv, cache_pool, l, sample_cfg, sample_cp, "s"),
        )
        new_x = []
        for x, c0, n0, m0, cc, pc, sl, cfg, cpcfg, tag in streams:
            qkv, og, u, p, sg, gates = _inproj(x, row2(norm_mix_g, l), w_main[l], wg_hi[l],
                                               wg_lo[l], gate_bias[l])
            h, c1, n1, m1 = _mlstm(qkv, og, gates, c0, n0, m0, sl, **cfg)
            zc, po, ncv, npl = _convpool(u, p, cc, pc, sl, *conv_w, **cpcfg)
            x1 = _merge(h, zc, po, sg, x, wm_b[l], wc_b[l], wp_b[l], row2(pool_scale, l), wo_b[l])
            x2 = _ffn(x1, row2(norm_ffn_g, l), wfg_b[l], wfu_b[l], wfd_b[l],
                      norm_final_g[None, :], final)
            new_x.append(x2)
            outs[tag + "C"].append(c1)
            outs[tag + "n"].append(n1[:, :, 0, :])
            outs[tag + "m"].append(m1[:, :, 0, 0])
            outs[tag + "cv"].append(ncv)
            outs[tag + "pl"].append(npl)
        xp, xs = new_x

    y_prompt = xp.reshape(bp, tp, D_MODEL)
    y_sample = xs.reshape(bs, SAMPLE_PAD, D_MODEL)[:, :ts]
    st = lambda name: jnp.stack(outs[name])
    return (y_prompt, y_sample,
            st("pC"), st("pn"), st("pm"), st("pcv"), st("ppl"),
            st("sC"), st("sn"), st("sm"), st("scv"), st("spl"))
```

```python
import functools

import jax
import jax.numpy as jnp
from jax import lax
from jax.experimental import pallas as pl
from jax.experimental.pallas import tpu as pltpu

f32 = jnp.float32
bf16 = jnp.bfloat16

D_MODEL = 1024
M_HEADS = 4
M_HEAD_DIM = 256
D_CONV = 512
CONV_WIDTH = 31
CONV_HIST = CONV_WIDTH - 1
POOL_WINDOWS = (2, 4, 8, 16)
POOL_GROUP_DIM = 128
POOL_OUT_DIM = 256
D_POOL = 512
POOL_HIST = 15
D_FF = 2816
PAST_LEN = 16384
NORM_EPS = 1e-6
LN_EPS = 1e-5

SUBLANES = 8
GATE_LANES = 128
SAMPLE_PAD = SUBLANES
ROW_TILE = 512
CONV_PAD = 32
POOL_PAD = 16
VMEM_LIMIT = 56 * 1024 * 1024

NT_DIMS = (((1,), (1,)), ((), ()))

_C_CV = 4 * D_MODEL
_C_PL = _C_CV + 2 * D_CONV
_C_GT = _C_PL + D_POOL
_W_MAIN = _C_GT + 3 * D_MODEL


def _dot(a, b):
    return jnp.dot(a, b, preferred_element_type=f32)


def _rms(x, g):
    return x * lax.rsqrt(jnp.mean(x * x, axis=-1, keepdims=True) + NORM_EPS) * g


def _resident(shape):
    zeros = (0,) * len(shape)
    return pl.BlockSpec(shape, lambda *_: zeros, pipeline_mode=pl.Buffered(1))


def _params(*sem):
    return pltpu.CompilerParams(dimension_semantics=sem, vmem_limit_bytes=VMEM_LIMIT)


def _inproj_body(x_ref, g_ref, w_ref, wgh_ref, wgl_ref, gb_ref,
                 qkv_ref, og_ref, u_ref, pl_ref, sg_ref, gates_ref):
    xn = _rms(x_ref[...], g_ref[...])
    xb = xn.astype(bf16)
    xl = (xn - xb.astype(f32)).astype(bf16)
    z = (_dot(xb, wgh_ref[...]) + _dot(xb, wgl_ref[...]) + _dot(xl, wgh_ref[...])
         + gb_ref[...])
    lane = lax.broadcasted_iota(jnp.int32, z.shape, 1)
    log_f = jnp.minimum(z, 0.0) - jnp.log1p(jnp.exp(-jnp.abs(z)))
    gates_ref[...] = jnp.where(lane < M_HEADS, z, log_f)

    for c in range(3):
        r = _dot(xb, w_ref[:, c * D_MODEL:(c + 1) * D_MODEL])
        if c == 1:
            r = r * (M_HEAD_DIM ** -0.5)
        qkv_ref[:, c * D_MODEL:(c + 1) * D_MODEL] = r.astype(bf16)
    og_ref[...] = jax.nn.sigmoid(_dot(xb, w_ref[:, 3 * D_MODEL:4 * D_MODEL])).astype(bf16)
    ca = _dot(xb, w_ref[:, _C_CV:_C_CV + D_CONV])
    cb = _dot(xb, w_ref[:, _C_CV + D_CONV:_C_PL])
    u_ref[...] = ca * jax.nn.sigmoid(cb)
    pl_ref[...] = _dot(xb, w_ref[:, _C_PL:_C_GT])
    for c in range(3):
        r = _dot(xb, w_ref[:, _C_GT + c * D_MODEL:_C_GT + (c + 1) * D_MODEL])
        sg_ref[:, c * D_MODEL:(c + 1) * D_MODEL] = jax.nn.sigmoid(r).astype(bf16)


def _inproj(x, g, w_main, wg_hi, wg_lo, gate_bias):
    rows = x.shape[0]
    tm = ROW_TILE
    row = lambda width: pl.BlockSpec((tm, width), lambda i: (i, 0))
    return pl.pallas_call(
        _inproj_body,
        grid=(rows // tm,),
        in_specs=[row(D_MODEL), _resident((1, D_MODEL)), _resident((D_MODEL, _W_MAIN)),
                  _resident((D_MODEL, GATE_LANES)), _resident((D_MODEL, GATE_LANES)),
                  _resident((1, GATE_LANES))],
        out_specs=[row(3 * D_MODEL), row(D_MODEL), row(D_CONV), row(D_POOL),
                   row(3 * D_MODEL), row(GATE_LANES)],
        out_shape=[jax.ShapeDtypeStruct((rows, 3 * D_MODEL), bf16),
                   jax.ShapeDtypeStruct((rows, D_MODEL), bf16),
                   jax.ShapeDtypeStruct((rows, D_CONV), f32),
                   jax.ShapeDtypeStruct((rows, D_POOL), f32),
                   jax.ShapeDtypeStruct((rows, 3 * D_MODEL), bf16),
                   jax.ShapeDtypeStruct((rows, GATE_LANES), f32)],
        compiler_params=_params("arbitrary"),
        name="inproj",
    )(x, g, w_main, wg_hi, wg_lo, gate_bias)


def _mlstm_body(*refs, n_seq, seq_rows, valid, n_chunks, heads):
    q_ref, k_ref, v_ref, og_ref, gates_ref, c0_ref, n0_ref, m0_ref = refs[:8]
    h_ref, c_ref, n_ref, m_ref = refs[-4:]
    rows = n_seq * seq_rows
    shift = seq_rows.bit_length() - 1
    hd = M_HEAD_DIM
    if n_chunks > 1:
        @pl.when(pl.program_id(2) == 0)
        def _():
            c_ref[...] = c0_ref[...]
            n_ref[...] = n0_ref[...]
            m_ref[...] = m0_ref[...]
        cs, ns, ms = c_ref, n_ref, m_ref
    else:
        cs, ns, ms = c0_ref, n0_ref, m0_ref

    ri = lax.broadcasted_iota(jnp.int32, (rows, rows), 0)
    ci = lax.broadcasted_iota(jnp.int32, (rows, rows), 1)
    if n_seq == 1:
        causal = ci <= ri
    else:
        same = (ri >> shift) == (ci >> shift)
        causal = same & (ci <= ri)
        eye = ri == ci
        last_col = same & ((ci & (seq_rows - 1)) == seq_rows - 1)
        row_id = lax.broadcasted_iota(jnp.int32, (rows, hd), 0) >> shift

    gts = gates_ref[...]
    lane = lax.broadcasted_iota(jnp.int32, gts.shape, 1)
    t_in = lax.broadcasted_iota(jnp.int32, gts.shape, 0) & (seq_rows - 1)
    ok = t_in < valid
    b_all = jnp.where(ok, gts, 0.0)
    step = 1
    while step < seq_rows:
        b_all = b_all + jnp.where(t_in >= step, pltpu.roll(b_all, step, 0), 0.0)
        step *= 2
    b_all = pltpu.roll(b_all, GATE_LANES - M_HEADS, 1)
    a_all = jnp.where(ok, gts, -jnp.inf) - b_all
    a_t = a_all.T

    for j in range(heads):
        cols = slice(j * hd, (j + 1) * hd)
        if heads == M_HEADS:
            a = a_all[:, j:j + 1]
            b = b_all[:, j:j + 1]
            a_row = a_t[j:j + 1, :]
        else:
            head = pl.program_id(1)
            a = jnp.sum(jnp.where(lane == head, a_all, 0.0), axis=1, keepdims=True)
            b = jnp.sum(jnp.where(lane == head, b_all, 0.0), axis=1, keepdims=True)
            sub = lax.broadcasted_iota(jnp.int32, (SUBLANES, rows), 0)
            a_row = jnp.sum(jnp.where(sub == head, a_t[0:SUBLANES], 0.0), axis=0, keepdims=True)

        amat = jnp.where(causal, a_row, -jnp.inf)
        cmax = jnp.max(amat, axis=1, keepdims=True)
        if n_seq == 1:
            m0_rows = ms[0, j, 0:1, 0:1]
        else:
            m0_rows = ms[:, j].reshape(rows, GATE_LANES)[:, 0:1]
        mx = jnp.maximum(m0_rows, cmax)
        m = b + mx
        w = jnp.exp(amat - mx)
        g = jnp.exp(m0_rows - mx)

        q = q_ref[:, cols]
        k = k_ref[:, cols]
        v = v_ref[:, cols]
        s = lax.dot_general(q, k, NT_DIMS, preferred_element_type=f32) * w
        intra = _dot(s.astype(bf16), v)
        den_intra = jnp.sum(s, axis=1, keepdims=True)

        d = -mx
        if n_seq == 1:
            d_last = d[rows - 1:rows]
        else:
            d_row = jnp.sum(jnp.where(eye, d, 0.0), axis=0, keepdims=True)
            d_last = jnp.sum(jnp.where(last_col, d_row, 0.0), axis=1, keepdims=True)
        decay = jnp.exp(d_last + a)
        qf = q.astype(f32)
        kf = k.astype(f32)
        vd_t = (decay * v.astype(f32)).T.astype(bf16)
        dk = decay * kf

        inter, qn = [], []
        for i in range(n_seq):
            lo, hi = i * seq_rows, (i + 1) * seq_rows
            c0 = cs[i, j]
            n0 = ns[i, j]
            m0 = ms[i, j, 0:1, 0:1]
            res = lax.dot_general(q, c0.astype(bf16), NT_DIMS, preferred_element_type=f32)
            inter.append(res[lo:hi])
            qn.append(jnp.sum(qf[lo:hi] * n0, axis=1, keepdims=True))
            keep = jnp.exp(d[hi - 1:hi] + m0)
            k_i = k if n_seq == 1 else jnp.where(row_id == i, kf, 0.0).astype(bf16)
            c_ref[i, j] = keep * c0 + _dot(vd_t, k_i)
            n_ref[i, j] = keep * n0 + jnp.sum(dk[lo:hi], axis=0, keepdims=True)
            m_ref[i, j] = jnp.broadcast_to(m[hi - 1:hi], (SUBLANES, GATE_LANES))
        inter = inter[0] if n_seq == 1 else jnp.concatenate(inter, axis=0)
        qn = qn[0] if n_seq == 1 else jnp.concatenate(qn, axis=0)

        num = g * inter + intra
        den = g * qn + den_intra
        hv = num / jnp.maximum(jnp.abs(den), jnp.exp(-m))
        h_ref[:, cols] = (og_ref[:, cols].astype(f32) * hv).astype(bf16)


def _mlstm(qkv, og, gates, c0, n0, m0, c_buf, *, state_layer, layer, depth, batch, n_seq,
           seq_rows, valid, n_chunks, heads):
    rows_total = qkv.shape[0]
    rows = n_seq * seq_rows
    nb = batch // n_seq
    hd = M_HEAD_DIM
    hblocks = M_HEADS // heads
    rmap = lambda off: (lambda b, h, c: (b * n_chunks + c, h + off * hblocks))
    in_state = lambda b, h, c: (state_layer, b, h, 0, 0)
    smap4 = lambda b, h, c: (b, h, 0, 0)
    body = functools.partial(_mlstm_body, n_seq=n_seq, seq_rows=seq_rows, valid=valid,
                             n_chunks=n_chunks, heads=heads)
    in_specs = [pl.BlockSpec((rows, heads * hd), rmap(0)),
                pl.BlockSpec((rows, heads * hd), rmap(1)),
                pl.BlockSpec((rows, heads * hd), rmap(2)),
                pl.BlockSpec((rows, heads * hd), rmap(0)),
                pl.BlockSpec((rows, GATE_LANES), lambda b, h, c: (b * n_chunks + c, 0)),
                pl.BlockSpec((None, n_seq, heads, hd, hd), in_state),
                pl.BlockSpec((None, n_seq, heads, 1, hd), in_state),
                pl.BlockSpec((n_seq, heads, SUBLANES, GATE_LANES), smap4)]
    args = [qkv, qkv, qkv, og, gates, c0, n0, m0]
    aliases = {}
    if c_buf is not None:
        in_specs.append(pl.BlockSpec(memory_space=pl.ANY))
        args.append(c_buf)
        aliases = {len(args) - 1: 1}
    return pl.pallas_call(
        body,
        grid=(nb, hblocks, n_chunks),
        in_specs=in_specs,
        out_specs=[pl.BlockSpec((rows, heads * hd), rmap(0)),
                   pl.BlockSpec((None, n_seq, heads, hd, hd), lambda b, h, c: (layer, b, h, 0, 0)),
                   pl.BlockSpec((n_seq, heads, 1, hd), smap4),
                   pl.BlockSpec((n_seq, heads, SUBLANES, GATE_LANES), smap4)],
        out_shape=[jax.ShapeDtypeStruct((rows_total, D_MODEL), bf16),
                   jax.ShapeDtypeStruct((depth, batch, M_HEADS, hd, hd), f32),
                   jax.ShapeDtypeStruct((batch, M_HEADS, 1, hd), f32),
                   jax.ShapeDtypeStruct((batch, M_HEADS, SUBLANES, GATE_LANES), f32)],
        input_output_aliases=aliases,
        compiler_params=_params("arbitrary", "arbitrary", "arbitrary"),
        name="mlstm",
    )(*args)


def _shift_up(x, r):
    return x if r == 0 else pltpu.roll(x, x.shape[0] - r, 0)


def _convpool_body(u_ref, p_ref, cc_ref, pc_ref, dww_ref, dwb_ref, lng_ref, lnb_ref,
                   zc_ref, po_ref, ncv_ref, npl_ref, ext_ref, pext_ref,
                   *, n_seq, seq_rows, valid, n_tiles, pos0):
    t = pl.program_id(1)
    conv_rows = min(seq_rows, 64)
    pool_rows = min(seq_rows, 128)
    first_tap = CONV_PAD - CONV_HIST
    for i in range(n_seq):
        r0 = i * seq_rows
        def _from_cache():
            ext_ref[pl.ds(first_tap, CONV_HIST), :] = cc_ref[i]
            pext_ref[pl.ds(POOL_PAD - POOL_HIST, POOL_HIST), :] = pc_ref[i]
            ext_ref[pl.ds(0, first_tap), :] = jnp.zeros((first_tap, D_CONV), f32)
            pext_ref[pl.ds(0, POOL_PAD - POOL_HIST), :] = jnp.zeros((POOL_PAD - POOL_HIST, D_POOL), f32)

        if n_tiles > 1:
            pl.when(t == 0)(_from_cache)

            @pl.when(t > 0)
            def _():
                ext_ref[pl.ds(0, CONV_PAD), :] = ext_ref[pl.ds(seq_rows, CONV_PAD), :]
                pext_ref[pl.ds(0, POOL_PAD), :] = pext_ref[pl.ds(seq_rows, POOL_PAD), :]
        else:
            _from_cache()
        ext_ref[pl.ds(CONV_PAD, seq_rows), :] = u_ref[pl.ds(r0, seq_rows), :]
        pext_ref[pl.ds(POOL_PAD, seq_rows), :] = p_ref[pl.ds(r0, seq_rows), :]

        for rs in range(0, seq_rows, conv_rows):
            window = ext_ref[pl.ds(rs, conv_rows + CONV_PAD), :]
            acc = jnp.broadcast_to(dwb_ref[...], (conv_rows, D_CONV))
            for r in range(SUBLANES):
                shifted = _shift_up(window, r)
                for off in range(first_tap, CONV_PAD + 1):
                    if off % SUBLANES == r:
                        j = off - first_tap
                        acc = acc + shifted[off - r:off - r + conv_rows] * dww_ref[j:j + 1, :]
            mu = jnp.mean(acc, axis=-1, keepdims=True)
            xc = acc - mu
            var = jnp.mean(xc * xc, axis=-1, keepdims=True)
            y = xc * lax.rsqrt(var + LN_EPS) * lng_ref[...] + lnb_ref[...]
            zc_ref[pl.ds(r0 + rs, conv_rows), :] = (y * jax.nn.sigmoid(y)).astype(bf16)

        for rs in range(0, seq_rows, pool_rows):
            pos = (pos0 + t * seq_rows + rs
                   + lax.broadcasted_iota(jnp.int32, (pool_rows, POOL_GROUP_DIM), 0))
            for gi, win in enumerate(POOL_WINDOWS):
                cols = pl.ds(gi * POOL_GROUP_DIM, POOL_GROUP_DIM)
                run = pext_ref[pl.ds(rs, pool_rows + POOL_PAD), cols]
                cur = run[POOL_PAD:]
                span = 1
                while span < win:
                    run = run + pltpu.roll(run, span, 0)
                    span *= 2
                cnt = jnp.minimum(pos + 1, win).astype(f32)
                po_ref[pl.ds(r0 + rs, pool_rows), cols] = (run[POOL_PAD:] / cnt - cur).astype(bf16)

        def _caches():
            ncv_ref[i] = ext_ref[pl.ds(first_tap + valid, CONV_HIST), :]
            npl_ref[i] = pext_ref[pl.ds(POOL_PAD - POOL_HIST + valid, POOL_HIST), :]
        if n_tiles > 1:
            pl.when(t == n_tiles - 1)(_caches)
        else:
            _caches()


def _convpool(u, p, cache_conv, cache_pool, layer, dww, dwb, lng, lnb,
              *, batch, n_seq, seq_rows, valid, n_tiles, pos0):
    rows_total = u.shape[0]
    rows = n_seq * seq_rows
    nb = batch // n_seq
    rmap = lambda b, t: (b * n_tiles + t, 0)
    body = functools.partial(_convpool_body, n_seq=n_seq, seq_rows=seq_rows, valid=valid,
                             n_tiles=n_tiles, pos0=pos0)
    return pl.pallas_call(
        body,
        grid=(nb, n_tiles),
        in_specs=[pl.BlockSpec((rows, D_CONV), rmap),
                  pl.BlockSpec((rows, D_POOL), rmap),
                  pl.BlockSpec((None, n_seq, CONV_HIST, D_CONV), lambda b, t: (layer, b, 0, 0)),
                  pl.BlockSpec((None, n_seq, POOL_HIST, D_POOL), lambda b, t: (layer, b, 0, 0)),
                  _resident((CONV_WIDTH, D_CONV)), _resident((1, D_CONV)),
                  _resident((1, D_CONV)), _resident((1, D_CONV))],
        out_specs=[pl.BlockSpec((rows, D_CONV), rmap),
                   pl.BlockSpec((rows, D_POOL), rmap),
                   pl.BlockSpec((n_seq, CONV_HIST, D_CONV), lambda b, t: (b, 0, 0)),
                   pl.BlockSpec((n_seq, POOL_HIST, D_POOL), lambda b, t: (b, 0, 0))],
        out_shape=[jax.ShapeDtypeStruct((rows_total, D_CONV), bf16),
                   jax.ShapeDtypeStruct((rows_total, D_POOL), bf16),
                   jax.ShapeDtypeStruct((batch, CONV_HIST, D_CONV), f32),
                   jax.ShapeDtypeStruct((batch, POOL_HIST, D_POOL), f32)],
        scratch_shapes=[pltpu.VMEM((CONV_PAD + seq_rows, D_CONV), f32),
                        pltpu.VMEM((POOL_PAD + seq_rows, D_POOL), f32)],
        compiler_params=_params("arbitrary", "arbitrary"),
        name="convpool",
    )(u, p, cache_conv, cache_pool, dww, dwb, lng, lnb)


def _merge_body(h_ref, zc_ref, po_ref, sg_ref, x_ref, wm_ref, wc_ref, wp_ref, ps_ref, wo_ref,
                o_ref):
    out_m = _dot(h_ref[...], wm_ref[...])
    merged = sg_ref[:, 0:D_MODEL].astype(f32) * out_m
    out_c = _dot(zc_ref[...], wc_ref[...])
    merged = merged + sg_ref[:, D_MODEL:2 * D_MODEL].astype(f32) * out_c
    out_p = jnp.concatenate(
        [_dot(po_ref[:, gi * POOL_GROUP_DIM:(gi + 1) * POOL_GROUP_DIM], wp_ref[gi])
         for gi in range(len(POOL_WINDOWS))], axis=1) * ps_ref[...]
    merged = merged + sg_ref[:, 2 * D_MODEL:3 * D_MODEL].astype(f32) * out_p
    o_ref[...] = x_ref[...] + _dot(merged.astype(bf16), wo_ref[...])


def _merge(h, zc, po, sg, x, wm, wc, wp, ps, wo):
    rows = x.shape[0]
    tm = ROW_TILE
    row = lambda width: pl.BlockSpec((tm, width), lambda i: (i, 0))
    return pl.pallas_call(
        _merge_body,
        grid=(rows // tm,),
        in_specs=[row(D_MODEL), row(D_CONV), row(D_POOL), row(3 * D_MODEL), row(D_MODEL),
                  _resident((D_MODEL, D_MODEL)), _resident((D_CONV, D_MODEL)),
                  _resident((len(POOL_WINDOWS), POOL_GROUP_DIM, POOL_OUT_DIM)),
                  _resident((1, D_MODEL)), _resident((D_MODEL, D_MODEL))],
        out_specs=row(D_MODEL),
        out_shape=jax.ShapeDtypeStruct((rows, D_MODEL), f32),
        compiler_params=_params("arbitrary"),
        name="merge",
    )(h, zc, po, sg, x, wm, wc, wp, ps, wo)


FF_CHUNK = D_FF // 2


def _ffn_body(x_ref, g_ref, wg_ref, wu_ref, wd_ref, gf_ref, o_ref, *, final):
    x = x_ref[...]
    xn = _rms(x, g_ref[...]).astype(bf16)
    acc = x
    for c in range(0, D_FF, FF_CHUNK):
        a = _dot(xn, wg_ref[:, c:c + FF_CHUNK])
        up = _dot(xn, wu_ref[:, c:c + FF_CHUNK])
        act = (a * jax.nn.sigmoid(a) * up).astype(bf16)
        acc = acc + _dot(act, wd_ref[c:c + FF_CHUNK, :])
    if final:
        acc = _rms(acc, gf_ref[...])
    o_ref[...] = acc


def _ffn(x, g, wg, wu, wd, gf, final):
    rows = x.shape[0]
    tm = ROW_TILE
    row = pl.BlockSpec((tm, D_MODEL), lambda i: (i, 0))
    return pl.pallas_call(
        functools.partial(_ffn_body, final=final),
        grid=(rows // tm,),
        in_specs=[row, _resident((1, D_MODEL)), _resident((D_MODEL, D_FF)),
                  _resident((D_MODEL, D_FF)), _resident((D_FF, D_MODEL)),
                  _resident((1, D_MODEL))],
        out_specs=row,
        out_shape=jax.ShapeDtypeStruct((rows, D_MODEL), f32),
        compiler_params=_params("arbitrary"),
        name="ffn",
    )(x, g, wg, wu, wd, gf)


PROMPT_CHUNK = 256
PROMPT_CONV_TILE = 512
SAMPLE_GROUP = 16


def kernel(x_prompt, x_sample, state_mlstm_C, state_mlstm_n, state_mlstm_m, cache_conv, cache_pool, norm_mix_g, w_in, b_igate, b_fgate, w_mlstm_out, conv_dw_w, conv_dw_b, conv_ln_g, conv_ln_b, w_conv_out, w_pool, pool_scale, w_out, norm_ffn_g, w_ffn_gate, w_ffn_up, w_ffn_down, norm_final_g):
    depth = w_in.shape[0]
    bp, tp, _ = x_prompt.shape
    bs, ts, _ = x_sample.shape
    hd = M_HEAD_DIM

    w_main = jnp.concatenate([w_in[:, :, :4 * D_MODEL], w_in[:, :, 4 * D_MODEL + 2 * M_HEADS:]],
                             axis=-1).astype(bf16)
    wg = jnp.pad(w_in[:, :, 4 * D_MODEL:4 * D_MODEL + 2 * M_HEADS],
                 ((0, 0), (0, 0), (0, GATE_LANES - 2 * M_HEADS)))
    wg_hi = wg.astype(bf16)
    wg_lo = (wg - wg_hi.astype(f32)).astype(bf16)
    gate_bias = jnp.pad(jnp.concatenate([b_igate, b_fgate], axis=-1),
                        ((0, 0), (0, GATE_LANES - 2 * M_HEADS)))[:, None, :]
    wm_b = w_mlstm_out.astype(bf16)
    wc_b = w_conv_out.astype(bf16)
    wp_b = w_pool.astype(bf16)
    wo_b = w_out.astype(bf16)
    wfg_b = w_ffn_gate.astype(bf16)
    wfu_b = w_ffn_up.astype(bf16)
    wfd_b = w_ffn_down.astype(bf16)
    row2 = lambda a, l: a[l][None, :]

    xp = x_prompt.reshape(bp * tp, D_MODEL)
    xs = jnp.pad(x_sample, ((0, 0), (0, SAMPLE_PAD - ts), (0, 0))).reshape(bs * SAMPLE_PAD, D_MODEL)

    zero_c = jnp.zeros((1, bp, M_HEADS, hd, hd), f32)
    zero_n = jnp.zeros((1, bp, M_HEADS, 1, hd), f32)
    zero_m = jnp.zeros((bp, M_HEADS, SUBLANES, GATE_LANES), f32)
    zero_cc = jnp.zeros((1, bp, CONV_HIST, D_CONV), f32)
    zero_pc = jnp.zeros((1, bp, POOL_HIST, D_POOL), f32)
    state_n5 = state_mlstm_n.reshape(depth, bs, M_HEADS, 1, hd)

    prompt_cfg = dict(depth=depth, batch=bp, n_seq=1, seq_rows=PROMPT_CHUNK, valid=PROMPT_CHUNK,
                      n_chunks=tp // PROMPT_CHUNK, heads=M_HEADS)
    sample_cfg = dict(depth=depth, batch=bs, n_seq=SAMPLE_GROUP, seq_rows=SAMPLE_PAD, valid=ts,
                      n_chunks=1, heads=1)
    prompt_cp = dict(batch=bp, n_seq=1, seq_rows=PROMPT_CONV_TILE, valid=PROMPT_CONV_TILE,
                     n_tiles=tp // PROMPT_CONV_TILE, pos0=0)
    sample_cp = dict(batch=bs, n_seq=SAMPLE_GROUP, seq_rows=SAMPLE_PAD, valid=ts, n_tiles=1,
                     pos0=PAST_LEN)

    outs = {name: [] for name in ("pn", "pm", "pcv", "ppl", "sn", "sm", "scv", "spl")}
    c_all = {"p": None, "s": None}
    for l in range(depth):
        final = l == depth - 1
        conv_w = (conv_dw_w[l], row2(conv_dw_b, l), row2(conv_ln_g, l), row2(conv_ln_b, l))
        streams = (
            (xp, zero_c, zero_n, zero_m, zero_cc, zero_pc, 0, prompt_cfg, prompt_cp, "p"),
            (xs, state_mlstm_C, state_n5,
             jnp.broadcast_to(state_mlstm_m[l][:, :, None, None],
                              (bs, M_HEADS, SUBLANES, GATE_LANES)),
             cache_conv, cache_pool, l, sample_cfg, sample_cp, "s"),
        )
        new_x = []
        for x, c0, n0, m0, cc, pc, sl, cfg, cpcfg, tag in streams:
            qkv, og, u, p, sg, gates = _inproj(x, row2(norm_mix_g, l), w_main[l], wg_hi[l],
                                               wg_lo[l], gate_bias[l])
            h, c_all[tag], n1, m1 = _mlstm(qkv, og, gates, c0, n0, m0, c_all[tag],
                                           state_layer=sl, layer=l, **cfg)
            zc, po, ncv, npl = _convpool(u, p, cc, pc, sl, *conv_w, **cpcfg)
            x1 = _merge(h, zc, po, sg, x, wm_b[l], wc_b[l], wp_b[l], row2(pool_scale, l), wo_b[l])
            x2 = _ffn(x1, row2(norm_ffn_g, l), wfg_b[l], wfu_b[l], wfd_b[l],
                      norm_final_g[None, :], final)
            new_x.append(x2)
            outs[tag + "n"].append(n1[:, :, 0, :])
            outs[tag + "m"].append(m1[:, :, 0, 0])
            outs[tag + "cv"].append(ncv)
            outs[tag + "pl"].append(npl)
        xp, xs = new_x

    y_prompt = xp.reshape(bp, tp, D_MODEL)
    y_sample = xs.reshape(bs, SAMPLE_PAD, D_MODEL)[:, :ts]
    st = lambda name: jnp.stack(outs[name])
    return (y_prompt, y_sample,
            c_all["p"], st("pn"), st("pm"), st("pcv"), st("ppl"),
            c_all["s"], st("sn"), st("sm"), st("scv"), st("spl"))
```
